```python
import math
import jax
import jax.numpy as jnp
from jax import lax
import numpy as np

D_MODEL = 1024
BATCH = 2
SEQ = 16384
DEPTH = 4

DN_HEADS = 4
DN_HEAD_DIM = 128
DN_WIDTH = DN_HEADS * DN_HEAD_DIM
DN_CONV = 4
DN_CHUNK = 64
S5_WIDTH = 512
S5_GROUP = 16
S5_GROUPS = S5_WIDTH // S5_GROUP
S5_STATE = 64
LRU_WIDTH = 512
LRU_BLOCKS = 8
LRU_BLOCK = LRU_WIDTH // LRU_BLOCKS
LRU_CONV = 4
LRU_C = 8.0
N_BRANCH = 3
BRANCH_WIDTH = 512
D_FF = 3 * D_MODEL
FFN_CONV = 3
DEEPNORM_ALPHA = (2.0 * DEPTH) ** 0.25
DEEPNORM_BETA = (8.0 * DEPTH) ** -0.25
LN_EPS = 1e-5
RMS_EPS = 1e-6
L2_EPS = 1e-6

IN_SIZES = (DN_WIDTH, DN_WIDTH, DN_WIDTH, DN_WIDTH, DN_HEADS, DN_HEADS, S5_WIDTH, LRU_WIDTH, LRU_WIDTH, N_BRANCH * D_MODEL)
D_IN = sum(IN_SIZES)

kernel_name = 'hybrid_deltanet_s5_rglru_deepnorm'


def causal_dwconv(x, w, b=None):
    k, c = w.shape
    y = lax.conv_general_dilated(x, w[:, None, :].astype(x.dtype), window_strides=(1,), padding=[(k - 1, 0)], dimension_numbers=('NWC', 'WIO', 'NWC'), feature_group_count=c)
    if b is not None:
        y = y + b.astype(x.dtype)
    return y


def layer_norm(x, g, b):
    xf = x.astype(jnp.float32)
    mu = jnp.mean(xf, -1, keepdims=True)
    var = jnp.mean(jnp.square(xf - mu), -1, keepdims=True)
    return ((xf - mu) * lax.rsqrt(var + LN_EPS) * g.astype(jnp.float32) + b.astype(jnp.float32)).astype(x.dtype)


def _affine_combine(left, right):
    a_l, b_l = left
    a_r, b_r = right
    return a_r * a_l, a_r * b_l + b_r


def linear_scan(a, b):
    return lax.associative_scan(_affine_combine, (a, b), axis=1)[1]


def l2norm(t):
    return t * lax.rsqrt(jnp.sum(jnp.square(t), -1, keepdims=True) + L2_EPS)


def gated_deltanet(q, k, v, z, beta_logit, decay_logit, conv_w, a_log, dt_bias, norm_w):
    out_dtype = z.dtype
    bsz, seq, _ = q.shape
    h, dh, c = DN_HEADS, DN_HEAD_DIM, DN_CHUNK
    n = seq // c
    f32 = jnp.float32
    qkv = jax.nn.silu(causal_dwconv(jnp.concatenate([q, k, v], axis=-1), conv_w))
    qkv = qkv.astype(f32).reshape(bsz, seq, 3, h, dh)
    q = l2norm(qkv[:, :, 0]) * (dh ** -0.5)
    k = l2norm(qkv[:, :, 1])
    v = qkv[:, :, 2]
    beta = jax.nn.sigmoid(beta_logit.astype(f32))
    g = -jnp.exp(a_log.astype(f32)) * jax.nn.softplus(decay_logit.astype(f32) + dt_bias.astype(f32))

    def to_chunks(t):
        return t.reshape(bsz, n, c, h, -1).transpose(0, 3, 1, 2, 4)

    qc, kc, vc = to_chunks(q), to_chunks(k), to_chunks(v)
    bc = to_chunks(beta[..., None])[..., 0]
    gc = jnp.cumsum(to_chunks(g[..., None])[..., 0], axis=-1)
    causal = jnp.tril(jnp.ones((c, c), dtype=bool))
    strict = jnp.tril(jnp.ones((c, c), dtype=bool), -1)
    decay = jnp.exp(jnp.where(causal, gc[..., :, None] - gc[..., None, :], -jnp.inf))
    kb = kc * bc[..., None]
    lower = jnp.where(strict, jnp.einsum('bhncd,bhnjd->bhncj', kb, kc) * decay, 0.0)
    tmat = lower + jnp.eye(c, dtype=f32)
    rhs = jnp.concatenate([vc * bc[..., None], kb * jnp.exp(gc)[..., None]], axis=-1)
    sol = lax.linalg.triangular_solve(tmat, rhs, left_side=True, lower=True, unit_diagonal=True)
    u_c, w_c = sol[..., :dh], sol[..., dh:]
    attn = jnp.einsum('bhncd,bhnjd->bhncj', qc, kc) * decay
    q_dec = qc * jnp.exp(gc)[..., None]
    k_tail = kc * jnp.exp(gc[..., -1:] - gc)[..., None]
    chunk_decay = jnp.exp(gc[..., -1])

    def step(state, inp):
        u_i, w_i, qd_i, a_i, kt_i, cd_i = inp
        v_new = u_i - jnp.einsum('bhcd,bhde->bhce', w_i, state)
        o_i = jnp.einsum('bhcd,bhde->bhce', qd_i, state) + jnp.einsum('bhcj,bhje->bhce', a_i, v_new)
        state = state * cd_i[..., None, None] + jnp.einsum('bhcd,bhce->bhde', kt_i, v_new)
        return state, o_i

    xs = tuple(jnp.moveaxis(t, 2, 0) for t in (u_c, w_c, q_dec, attn, k_tail, chunk_decay))
    _, o = lax.scan(step, jnp.zeros((bsz, h, dh, dh), f32), xs)
    o = o.transpose(1, 0, 3, 2, 4).reshape(bsz, seq, h, dh)
    o = o * lax.rsqrt(jnp.mean(jnp.square(o), -1, keepdims=True) + RMS_EPS) * norm_w.astype(f32)
    o = o * jax.nn.silu(z.astype(f32).reshape(bsz, seq, h, dh))
    return o.reshape(bsz, seq, DN_WIDTH).astype(out_dtype)


def s5_branch(u, lam_re, lam_im, log_step, b_re, b_im, c_re, c_im, d_skip, w_glu, b_glu):
    bsz, seq, _ = u.shape
    f32 = jnp.float32
    uf = u.astype(f32).reshape(bsz, seq, S5_GROUPS, S5_GROUP)
    lam = lax.complex(lam_re.astype(f32), lam_im.astype(f32))
    delta = jnp.exp(log_step.astype(f32))[:, None]
    lam_bar = jnp.exp(lam * delta)
    b_bar = ((lam_bar - 1.0) / lam)[..., None] * lax.complex(b_re.astype(f32), b_im.astype(f32))
    bu = jnp.einsum('gpc,bsgc->bsgp', b_bar, uf.astype(jnp.complex64))
    states = linear_scan(jnp.broadcast_to(lam_bar, bu.shape), bu)
    cmat = lax.complex(c_re.astype(f32), c_im.astype(f32))
    y = jnp.einsum('gcp,bsgp->bsgc', cmat, states).real + d_skip.astype(f32) * uf
    y = jax.nn.gelu(y.reshape(bsz, seq, S5_WIDTH))
    y = y * jax.nn.sigmoid(y @ w_glu.astype(f32) + b_glu.astype(f32))
    return y.astype(u.dtype)


def rglru_branch(xb, yb, conv_w, conv_b, w_a, b_a, w_x, b_x, lam):
    bsz, seq, _ = xb.shape
    f32 = jnp.float32
    xc = causal_dwconv(xb, conv_w, conv_b).astype(f32).reshape(bsz, seq, LRU_BLOCKS, LRU_BLOCK)
    r = jax.nn.sigmoid(jnp.einsum('bsnc,ncd->bsnd', xc, w_a.astype(f32)) + b_a.astype(f32))
    i = jax.nn.sigmoid(jnp.einsum('bsnc,ncd->bsnd', xc, w_x.astype(f32)) + b_x.astype(f32))
    log_a = LRU_C * r * jax.nn.log_sigmoid(lam.astype(f32).reshape(LRU_BLOCKS, LRU_BLOCK))
    a = jnp.exp(log_a)
    mult = jnp.sqrt(-jnp.expm1(2.0 * log_a))
    mult = jnp.where((jnp.arange(seq) == 0)[None, :, None, None], 1.0, mult)
    hseq = linear_scan(a, mult * i * xc).reshape(bsz, seq, LRU_WIDTH)
    return (hseq * jax.nn.gelu(yb.astype(f32))).astype(xb.dtype)


def conv_geglu_ffn(x, w_up, conv_w, conv_b, w_down):
    hid = causal_dwconv(x @ w_up, conv_w, conv_b)
    gate, val = jnp.split(hid, 2, axis=-1)
    return (jax.nn.gelu(gate) * val) @ w_down


def setup_inputs(seed: int = 0):
    key = jax.random.key(seed)
    ks = iter(jax.random.split(key, 40))
    f32 = jnp.float32
    L = DEPTH

    def nrm(shape, scale):
        return jax.random.normal(next(ks), shape, f32) * scale

    def unif(shape, lo, hi):
        return jax.random.uniform(next(ks), shape, f32, lo, hi)

    x = nrm((BATCH, SEQ, D_MODEL), 1.0)
    w_in = nrm((L, D_MODEL, D_IN), D_MODEL ** -0.5)
    dn_conv_w = nrm((L, DN_CONV, 3 * DN_WIDTH), DN_CONV ** -0.5)
    dn_a_log = jnp.log(unif((L, DN_HEADS), 1.0, 16.0))
    dt = jnp.exp(unif((L, DN_HEADS), math.log(1e-3), math.log(1e-1)))
    dn_dt_bias = dt + jnp.log(-jnp.expm1(-dt))
    dn_norm_w = 1.0 + nrm((L, DN_HEAD_DIM), 0.02)
    s5_lam_re = -0.5 + nrm((L, S5_GROUPS, S5_STATE), 0.01)
    s5_lam_im = jnp.tile(jnp.pi * jnp.arange(S5_STATE, dtype=f32), (L, S5_GROUPS, 1))
    s5_log_step = unif((L, S5_GROUPS), math.log(1e-3), math.log(1e-1))
    s5_b_re = nrm((L, S5_GROUPS, S5_STATE, S5_GROUP), (2.0 * S5_GROUP) ** -0.5)
    s5_b_im = nrm((L, S5_GROUPS, S5_STATE, S5_GROUP), (2.0 * S5_GROUP) ** -0.5)
    s5_c_re = nrm((L, S5_GROUPS, S5_GROUP, S5_STATE), (2.0 * S5_STATE) ** -0.5)
    s5_c_im = nrm((L, S5_GROUPS, S5_GROUP, S5_STATE), (2.0 * S5_STATE) ** -0.5)
    s5_d = nrm((L, S5_GROUPS, S5_GROUP), 1.0)
    s5_w_glu = nrm((L, S5_WIDTH, S5_WIDTH), S5_WIDTH ** -0.5)
    s5_b_glu = nrm((L, S5_WIDTH), 0.01)
    lru_conv_w = nrm((L, LRU_CONV, LRU_WIDTH), LRU_CONV ** -0.5)
    lru_conv_b = nrm((L, LRU_WIDTH), 0.01)
    lru_w_a = nrm((L, LRU_BLOCKS, LRU_BLOCK, LRU_BLOCK), LRU_BLOCK ** -0.5)
    lru_b_a = nrm((L, LRU_BLOCKS, LRU_BLOCK), 0.01)
    lru_w_x = nrm((L, LRU_BLOCKS, LRU_BLOCK, LRU_BLOCK), LRU_BLOCK ** -0.5)
    lru_b_x = nrm((L, LRU_BLOCKS, LRU_BLOCK), 0.01)
    a0 = unif((L, LRU_WIDTH), 0.9, 0.999) ** (1.0 / LRU_C)
    lru_lam = jnp.log(a0) - jnp.log1p(-a0)
    w_branch = nrm((L, N_BRANCH, BRANCH_WIDTH, D_MODEL), BRANCH_WIDTH ** -0.5)
    b_gate = nrm((L, N_BRANCH, D_MODEL), 0.01)
    w_out = nrm((L, D_MODEL, D_MODEL), D_MODEL ** -0.5 * DEEPNORM_BETA)
    ln1_g = 1.0 + nrm((L, D_MODEL), 0.02)
    ln1_b = nrm((L, D_MODEL), 0.01)
    ffn_w_up = nrm((L, D_MODEL, 2 * D_FF), D_MODEL ** -0.5)
    ffn_conv_w = nrm((L, FFN_CONV, 2 * D_FF), FFN_CONV ** -0.5)
    ffn_conv_b = nrm((L, 2 * D_FF), 0.01)
    ffn_w_down = nrm((L, D_FF, D_MODEL), D_FF ** -0.5 * DEEPNORM_BETA)
    ln2_g = 1.0 + nrm((L, D_MODEL), 0.02)
    ln2_b = nrm((L, D_MODEL), 0.01)
    return {'x': x, 'w_in': w_in, 'dn_conv_w': dn_conv_w, 'dn_a_log': dn_a_log, 'dn_dt_bias': dn_dt_bias, 'dn_norm_w': dn_norm_w, 's5_lam_re': s5_lam_re, 's5_lam_im': s5_lam_im, 's5_log_step': s5_log_step, 's5_b_re': s5_b_re, 's5_b_im': s5_b_im, 's5_c_re': s5_c_re, 's5_c_im': s5_c_im, 's5_d': s5_d, 's5_w_glu': s5_w_glu, 's5_b_glu': s5_b_glu, 'lru_conv_w': lru_conv_w, 'lru_conv_b': lru_conv_b, 'lru_w_a': lru_w_a, 'lru_b_a': lru_b_a, 'lru_w_x': lru_w_x, 'lru_b_x': lru_b_x, 'lru_lam': lru_lam, 'w_branch': w_branch, 'b_gate': b_gate, 'w_out': w_out, 'ln1_g': ln1_g, 'ln1_b': ln1_b, 'ffn_w_up': ffn_w_up, 'ffn_conv_w': ffn_conv_w, 'ffn_conv_b': ffn_conv_b, 'ffn_w_down': ffn_w_down, 'ln2_g': ln2_g, 'ln2_b': ln2_b}


def reference(x, w_in, dn_conv_w, dn_a_log, dn_dt_bias, dn_norm_w, s5_lam_re, s5_lam_im, s5_log_step, s5_b_re, s5_b_im, s5_c_re, s5_c_im, s5_d, s5_w_glu, s5_b_glu, lru_conv_w, lru_conv_b, lru_w_a, lru_b_a, lru_w_x, lru_b_x, lru_lam, w_branch, b_gate, w_out, ln1_g, ln1_b, ffn_w_up, ffn_conv_w, ffn_conv_b, ffn_w_down, ln2_g, ln2_b):
    bsz, seq, _ = x.shape
    offsets = np.cumsum(IN_SIZES)[:-1].tolist()
    for l in range(DEPTH):
        proj = x @ w_in[l]
        dq, dk, dv, dz, dbeta, ddecay, s5_u, lru_x, lru_y, gate_logits = jnp.split(proj, offsets, axis=-1)
        o_dn = gated_deltanet(dq, dk, dv, dz, dbeta, ddecay, dn_conv_w[l], dn_a_log[l], dn_dt_bias[l], dn_norm_w[l])
        o_s5 = s5_branch(s5_u, s5_lam_re[l], s5_lam_im[l], s5_log_step[l], s5_b_re[l], s5_b_im[l], s5_c_re[l], s5_c_im[l], s5_d[l], s5_w_glu[l], s5_b_glu[l])
        o_lru = rglru_branch(lru_x, lru_y, lru_conv_w[l], lru_conv_b[l], lru_w_a[l], lru_b_a[l], lru_w_x[l], lru_b_x[l], lru_lam[l])
        branches = jnp.stack([o_dn, o_s5, o_lru], axis=2)
        branch_d = jnp.einsum('bsnc,ncd->bsnd', branches, w_branch[l])
        gates = jax.nn.sigmoid(gate_logits.reshape(bsz, seq, N_BRANCH, D_MODEL) + b_gate[l])
        mixed = jnp.sum(gates * branch_d, axis=2) @ w_out[l]
        x = layer_norm(DEEPNORM_ALPHA * x + mixed, ln1_g[l], ln1_b[l])
        f = conv_geglu_ffn(x, ffn_w_up[l], ffn_conv_w[l], ffn_conv_b[l], ffn_w_down[l])
        x = layer_norm(DEEPNORM_ALPHA * x + f, ln2_g[l], ln2_b[l])
    return x
```

```python
import functools
import math

import jax
import jax.numpy as jnp
from jax import lax
from jax.experimental import pallas as pl
from jax.experimental.pallas import tpu as pltpu

F32 = jnp.float32
BF16 = jnp.bfloat16

D_MODEL = 1024
DN_HEADS = 4
DN_HEAD_DIM = 128
DN_WIDTH = DN_HEADS * DN_HEAD_DIM
DN_CHUNK = 64
S5_WIDTH = 512
S5_GROUP = 16
S5_GROUPS = 32
S5_STATE = 64
LRU_WIDTH = 512
LRU_C = 8.0
N_BRANCH = 3
D_FF = 3 * D_MODEL
IN_SIZES = (DN_WIDTH, DN_WIDTH, DN_WIDTH, DN_WIDTH, DN_HEADS, DN_HEADS, S5_WIDTH, LRU_WIDTH, LRU_WIDTH, N_BRANCH * D_MODEL)
LN_EPS = 1e-5
RMS_EPS = 1e-6
L2_EPS = 1e-6

SUBLANES = 8
LANES = 128
VMEM_LIMIT = 56 * 1024 * 1024

PROJ_TM = 512
DN_TB = 256
S5_TB = 256
S5_SLAB = 40
LRU_TB = 512
MERGE_TM = 512
FFN_TM = 512
FFN_CK = 512


def _cparams(sem):
    return pltpu.CompilerParams(dimension_semantics=sem, vmem_limit_bytes=VMEM_LIMIT)


def _resident(shape):
    nd = len(shape)
    return pl.BlockSpec(shape, lambda *_: (0,) * nd, pipeline_mode=pl.Buffered(1))


def _sigmoid(x):
    return jax.nn.sigmoid(x)


def _gelu(x):
    return jax.nn.gelu(x)


def _softplus(x):
    return jnp.maximum(x, 0.0) + jnp.log1p(jnp.exp(-jnp.abs(x)))


def _dot(a, b):
    return jnp.dot(a, b, preferred_element_type=F32)


def _dot_nt(a, b):
    return lax.dot_general(a, b, (((1,), (1,)), ((), ())), preferred_element_type=F32)


def _layer_norm(y, g, b):
    mu = jnp.mean(y, axis=-1, keepdims=True)
    d = y - mu
    var = jnp.mean(d * d, axis=-1, keepdims=True)
    return d * lax.rsqrt(var + LN_EPS) * g + b


def _proj_body(x_ref, w_ref, qkvz_ref, s5u_ref, lru_ref, gate_ref, bd_ref):
    xb = x_ref[...].astype(BF16)
    col = 0
    for ref in (qkvz_ref, s5u_ref, lru_ref, gate_ref):
        n = ref.shape[1]
        for c in range(0, n, 512):
            ref[:, c:c + 512] = _dot(xb, w_ref[:, col + c:col + c + 512]).astype(ref.dtype)
        col += n
    bd_ref[...] = _dot(xb, w_ref[:, col:col + LANES])


def _proj(x2, w_packed):
    t = x2.shape[0]
    tm = min(PROJ_TM, t)
    widths = (4 * DN_WIDTH, S5_WIDTH, 2 * LRU_WIDTH, N_BRANCH * D_MODEL)
    out_shape = [jax.ShapeDtypeStruct((t, n), BF16) for n in widths] + [jax.ShapeDtypeStruct((t, LANES), F32)]
    out_specs = [pl.BlockSpec((tm, n), lambda i: (i, 0)) for n in widths] + [pl.BlockSpec((tm, LANES), lambda i: (i, 0))]
    return pl.pallas_call(
        _proj_body,
        grid=(t // tm,),
        in_specs=[pl.BlockSpec((tm, D_MODEL), lambda i: (i, 0)), _resident(w_packed.shape)],
        out_specs=out_specs,
        out_shape=out_shape,
        compiler_params=_cparams(("arbitrary",)),
        name="proj",
    )(x2, w_packed)


def _dn_body(qkvz_ref, bd_ref, cw_ref, alog_ref, dtb_ref, nw_ref, o_ref, xbuf, s_ref):
    tb = qkvz_ref.shape[0]
    nchunk = tb // DN_CHUNK
    dh = DN_HEAD_DIM

    @pl.when(pl.program_id(1) == 0)
    def _():
        xbuf[0:SUBLANES, :] = jnp.zeros((SUBLANES, 3 * DN_WIDTH), F32)
        s_ref[...] = jnp.zeros(s_ref.shape, F32)

    xbuf[SUBLANES:SUBLANES + tb, :] = qkvz_ref[:, 0:3 * DN_WIDTH].astype(F32)

    def conv_silu(c0):
        y = cw_ref[3:4, c0:c0 + dh] * xbuf[SUBLANES:SUBLANES + tb, c0:c0 + dh]
        for k in range(3):
            off = SUBLANES - 3 + k
            y = y + cw_ref[k:k + 1, c0:c0 + dh] * xbuf[off:off + tb, c0:c0 + dh]
        return y * _sigmoid(y)

    bd = bd_ref[...]
    g = -jnp.exp(alog_ref[...]) * _softplus(bd + dtb_ref[...])
    r = lax.broadcasted_iota(jnp.int32, (tb, tb), 0)
    c = lax.broadcasted_iota(jnp.int32, (tb, tb), 1)
    same = (r >> 6) == (c >> 6)
    incl = same & (r >= c)
    strict = same & (r > c)
    eye = (r == c).astype(F32)
    g_t = g.T
    upper = (same & (r <= c)).astype(F32)
    gc_row = jnp.dot(g_t, upper, preferred_element_type=F32, precision=lax.Precision.HIGHEST)
    gl_row = jnp.dot(g_t, same.astype(F32), preferred_element_type=F32, precision=lax.Precision.HIGHEST)
    gc_col = gc_row.T
    gl_col = gl_row.T

    for h in range(DN_HEADS):
        qh = conv_silu(h * dh)
        kh = conv_silu(DN_WIDTH + h * dh)
        vh = conv_silu(2 * DN_WIDTH + h * dh)
        qh = qh * lax.rsqrt(jnp.sum(qh * qh, axis=-1, keepdims=True) + L2_EPS) * (dh ** -0.5)
        kh = kh * lax.rsqrt(jnp.sum(kh * kh, axis=-1, keepdims=True) + L2_EPS)
        beta = _sigmoid(bd[:, h:h + 1])
        gcol = gc_col[:, DN_HEADS + h:DN_HEADS + h + 1]
        glcol = gl_col[:, DN_HEADS + h:DN_HEADS + h + 1]
        grow = gc_row[DN_HEADS + h:DN_HEADS + h + 1, :]
        decay = jnp.exp(jnp.where(incl, gcol - grow, -1e30))
        egc = jnp.exp(gcol)
        kb = kh * beta
        kbf = kh.astype(BF16)
        a_mat = jnp.where(strict, _dot_nt(kb.astype(BF16), kbf) * decay, 0.0)
        attn = (_dot_nt(qh.astype(BF16), kbf) * decay).astype(BF16)

        m1 = -a_mat
        m1b = m1.astype(BF16)
        mk = _dot(m1b, m1b)
        q_sum = eye + m1
        for _ in range(4):
            mkb = mk.astype(BF16)
            rr = _dot(mkb, jnp.concatenate([mkb, q_sum.astype(BF16)], axis=1))
            mk = rr[:, :tb]
            q_sum = q_sum + rr[:, tb:]
        q_sum = q_sum + _dot(mk.astype(BF16), q_sum.astype(BF16))
        rhs = jnp.concatenate([vh * beta, kb * egc], axis=1).astype(BF16)
        uw = _dot(q_sum.astype(BF16), rhs)

        qd = (qh * egc).astype(BF16)
        kt_t = (kh * jnp.exp(glcol - gcol)).T.astype(BF16)
        state = s_ref[h]
        outs = []
        for ci in range(nchunk):
            lo = ci * DN_CHUNK
            hi = lo + DN_CHUNK
            sb = state.astype(BF16)
            ws = _dot(jnp.concatenate([uw[lo:hi, dh:].astype(BF16), qd[lo:hi]], axis=0), sb)
            v_new = uw[lo:hi, :dh] - ws[:DN_CHUNK]
            pieces = []
            if lo > 0:
                pieces.append(jnp.zeros((lo, dh), F32))
            pieces.append(v_new)
            if hi < tb:
                pieces.append(jnp.zeros((tb - hi, dh), F32))
            v_full = jnp.concatenate(pieces, axis=0).astype(BF16)
            outs.append(ws[DN_CHUNK:] + _dot(attn[lo:hi, :], v_full))
            cd = jnp.exp(gl_col[lo:lo + 1, DN_HEADS + h:DN_HEADS + h + 1])
            state = state * cd + _dot(kt_t, v_full)
        s_ref[h] = state
        o = jnp.concatenate(outs, axis=0)
        o = o * lax.rsqrt(jnp.mean(o * o, axis=-1, keepdims=True) + RMS_EPS) * nw_ref[...]
        z = qkvz_ref[:, 3 * DN_WIDTH + h * dh:3 * DN_WIDTH + (h + 1) * dh].astype(F32)
        o_ref[:, h * dh:(h + 1) * dh] = (o * (z * _sigmoid(z))).astype(o_ref.dtype)
    xbuf[0:SUBLANES, :] = xbuf[tb:tb + SUBLANES, :]


def _deltanet(qkvz, bd, conv_w, alog_row, dtb_row, norm_w):
    b, s, _ = qkvz.shape
    tb = min(DN_TB, s)
    small = lambda shape: pl.BlockSpec(shape, lambda bi, j: (0,) * len(shape))
    return pl.pallas_call(
        _dn_body,
        grid=(b, s // tb),
        in_specs=[
            pl.BlockSpec((None, tb, 4 * DN_WIDTH), lambda bi, j: (bi, j, 0)),
            pl.BlockSpec((None, tb, LANES), lambda bi, j: (bi, j, 0)),
            small(conv_w.shape), small(alog_row.shape), small(dtb_row.shape), small(norm_w.shape),
        ],
        out_specs=pl.BlockSpec((None, tb, DN_WIDTH), lambda bi, j: (bi, j, 0)),
        out_shape=jax.ShapeDtypeStruct((b, s, DN_WIDTH), BF16),
        scratch_shapes=[
            pltpu.VMEM((SUBLANES + tb, 3 * DN_WIDTH), F32),
            pltpu.VMEM((DN_HEADS, DN_HEAD_DIM, DN_HEAD_DIM), F32),
        ],
        compiler_params=_cparams(("arbitrary", "arbitrary")),
        name="dnet",
    )(qkvz, bd, conv_w, alog_row, dtb_row, norm_w)


def _s5_prep_body(lre_ref, lim_ref, ls_ref, bre_ref, bim_ref, exp_ref, lbr_ref, lbi_ref, bbr_ref, bbi_ref):
    lre = lre_ref[...]
    lim = lim_ref[...]
    delta = jnp.exp(ls_ref[...])
    mag = jnp.exp(lre * delta)
    lbr = mag * jnp.cos(lim * delta)
    lbi = mag * jnp.sin(lim * delta)
    lbr_ref[...] = lbr
    lbi_ref[...] = lbi
    nr = lbr - 1.0
    den = lre * lre + lim * lim
    cr = (nr * lre + lbi * lim) / den
    ci = (lbi * lre - nr * lim) / den
    hp = lax.Precision.HIGHEST
    cr_x = jnp.dot(cr, exp_ref[...], preferred_element_type=F32, precision=hp)
    ci_x = jnp.dot(ci, exp_ref[...], preferred_element_type=F32, precision=hp)
    bre = bre_ref[...]
    bim = bim_ref[...]
    bbr_ref[...] = cr_x * bre - ci_x * bim
    bbi_ref[...] = cr_x * bim + ci_x * bre


def _s5_prep(lam_re, lam_im, log_step, b_re, b_im):
    nl = lam_re.shape[0]
    rows = nl * S5_GROUPS
    lre = lam_re.reshape(rows, S5_STATE)
    lim = lam_im.reshape(rows, S5_STATE)
    ls = log_step.reshape(rows, 1)
    bre = b_re.reshape(rows, S5_STATE * S5_GROUP)
    bim = b_im.reshape(rows, S5_STATE * S5_GROUP)
    expand = jnp.repeat(jnp.eye(S5_STATE, dtype=F32), S5_GROUP, axis=1)
    outs = pl.pallas_call(
        _s5_prep_body,
        out_shape=[jax.ShapeDtypeStruct((rows, S5_STATE), F32)] * 2 + [jax.ShapeDtypeStruct((rows, S5_STATE * S5_GROUP), F32)] * 2,
        name="s5_prep",
    )(lre, lim, ls, bre, bim, expand)
    lbr, lbi, bbr, bbi = outs
    shp = (nl, S5_GROUPS, S5_STATE)
    return lbr.reshape(shp), lbi.reshape(shp), bbr.reshape(shp + (S5_GROUP,)), bbi.reshape(shp + (S5_GROUP,))


def _s5_body(u_ref, wb_ref, cw_ref, lr_ref, li_ref, d_ref, wg_ref, bg_ref, o_ref, sre, sim, hre, him):
    nb, tb, _ = u_ref.shape
    half = S5_WIDTH // 2
    ntile = 8

    @pl.when(pl.program_id(0) == 0)
    def _():
        hre[...] = jnp.zeros(hre.shape, F32)
        him[...] = jnp.zeros(him.shape, F32)

    for b in range(nb):
        for hf in range(2):
            bu = _dot(u_ref[b, :, hf * half:(hf + 1) * half], wb_ref[hf])
            for k in range(ntile):
                row = b * 2 * ntile + hf * ntile + k
                sre[pl.ds(row, tb, stride=S5_SLAB), :] = bu[:, k * LANES:(k + 1) * LANES]
                sim[pl.ds(row, tb, stride=S5_SLAB), :] = bu[:, (ntile + k) * LANES:(ntile + k + 1) * LANES]

    lr = lr_ref[...]
    li = li_ref[...]
    nrow = nb * 2 * ntile

    def step(t, carry):
        hr, hi = carry
        base = pl.multiple_of(t * S5_SLAB, SUBLANES)
        br = sre[pl.ds(base, nrow), :]
        bi = sim[pl.ds(base, nrow), :]
        nr = lr * hr - li * hi + br
        ni = lr * hi + li * hr + bi
        sre[pl.ds(base, nrow), :] = nr
        sim[pl.ds(base, nrow), :] = ni
        return nr, ni

    hr, hi = lax.fori_loop(0, tb, step, (hre[...], him[...]), unroll=4)
    hre[...] = hr
    him[...] = hi

    for b in range(nb):
        ys = []
        for hf in range(2):
            tiles = [sre[pl.ds(b * 2 * ntile + hf * ntile + k, tb, stride=S5_SLAB), :] for k in range(ntile)]
            tiles += [sim[pl.ds(b * 2 * ntile + hf * ntile + k, tb, stride=S5_SLAB), :] for k in range(ntile)]
            hcat = jnp.concatenate(tiles, axis=1).astype(BF16)
            ys.append(_dot(hcat, cw_ref[hf]))
        y = jnp.concatenate(ys, axis=1) + d_ref[...] * u_ref[b].astype(F32)
        y = _gelu(y)
        gate = _sigmoid(_dot(y.astype(BF16), wg_ref[...]) + bg_ref[...])
        o_ref[b] = (y * gate).astype(o_ref.dtype)


def _s5(u, wb, cw, lr, li, d_row, w_glu, b_glu):
    b, s, _ = u.shape
    tb = min(S5_TB, s)
    nrow = b * 16
    return pl.pallas_call(
        _s5_body,
        grid=(s // tb,),
        in_specs=[
            pl.BlockSpec((b, tb, S5_WIDTH), lambda j: (0, j, 0)),
            _resident(wb.shape), _resident(cw.shape), _resident(lr.shape), _resident(li.shape),
            _resident(d_row.shape), _resident(w_glu.shape), _resident(b_glu.shape),
        ],
        out_specs=pl.BlockSpec((b, tb, S5_WIDTH), lambda j: (0, j, 0)),
        out_shape=jax.ShapeDtypeStruct((b, s, S5_WIDTH), BF16),
        scratch_shapes=[
            pltpu.VMEM((tb * S5_SLAB, LANES), F32),
            pltpu.VMEM((tb * S5_SLAB, LANES), F32),
            pltpu.VMEM((nrow, LANES), F32),
            pltpu.VMEM((nrow, LANES), F32),
        ],
        compiler_params=_cparams(("arbitrary",)),
        name="s5",
    )(u, wb, cw, lr, li, d_row, w_glu, b_glu)


def _lru_body(xy_ref, cw_ref, cb_ref, wa_ref, ba_ref, wx_ref, bx_ref, lam_ref, o_ref, xbuf, a_s, b_s, hcar):
    nb, tb, _ = xy_ref.shape
    w = LRU_WIDTH
    ntile = w // LANES
    rows_per_step = nb * ntile
    first_step = pl.program_id(0) == 0

    @pl.when(first_step)
    def _():
        xbuf[:, 0:SUBLANES, :] = jnp.zeros((nb, SUBLANES, w), F32)
        hcar[...] = jnp.zeros(hcar.shape, F32)

    lam = lam_ref[...]
    log_sig = -_softplus(-lam)
    is_first = first_step & (lax.broadcasted_iota(jnp.int32, (tb, 1), 0) == 0)
    for b in range(nb):
        x = xy_ref[b, :, 0:w].astype(F32)
        xbuf[b, SUBLANES:SUBLANES + tb, :] = x
        xc = cw_ref[3:4, :] * x + cb_ref[...]
        for k in range(3):
            off = SUBLANES - 3 + k
            xc = xc + cw_ref[k:k + 1, :] * xbuf[b, off:off + tb, :]
        xbuf[b, 0:SUBLANES, :] = xbuf[b, tb:tb + SUBLANES, :]
        xcb = xc.astype(BF16)
        rg = _sigmoid(_dot(xcb, wa_ref[...]) + ba_ref[...])
        ig = _sigmoid(_dot(xcb, wx_ref[...]) + bx_ref[...])
        log_a = LRU_C * rg * log_sig
        a = jnp.exp(log_a)
        mult = jnp.sqrt(1.0 - a * a)
        mult = jnp.where(is_first, 1.0, mult)
        bb = mult * ig * xc
        for ct in range(ntile):
            a_s[pl.ds(b * ntile + ct, tb, stride=rows_per_step), :] = a[:, ct * LANES:(ct + 1) * LANES]
            b_s[pl.ds(b * ntile + ct, tb, stride=rows_per_step), :] = bb[:, ct * LANES:(ct + 1) * LANES]

    def step(t, h):
        base = pl.multiple_of(t * rows_per_step, rows_per_step)
        h = a_s[pl.ds(base, rows_per_step), :] * h + b_s[pl.ds(base, rows_per_step), :]
        b_s[pl.ds(base, rows_per_step), :] = h
        return h

    hcar[...] = lax.fori_loop(0, tb, step, hcar[...], unroll=8)

    for b in range(nb):
        for ct in range(ntile):
            hseq = b_s[pl.ds(b * ntile + ct, tb, stride=rows_per_step), :]
            y = xy_ref[b, :, w + ct * LANES:w + (ct + 1) * LANES].astype(F32)
            o_ref[b, :, ct * LANES:(ct + 1) * LANES] = (hseq * _gelu(y)).astype(o_ref.dtype)


def _lru(xy, conv_w, conv_b, wa, ba, wx, bx, lam):
    b, s, _ = xy.shape
    tb = min(LRU_TB, s)
    rows = b * (LRU_WIDTH // LANES)
    return pl.pallas_call(
        _lru_body,
        grid=(s // tb,),
        in_specs=[
            pl.BlockSpec((b, tb, 2 * LRU_WIDTH), lambda j: (0, j, 0)),
            _resident(conv_w.shape), _resident(conv_b.shape), _resident(wa.shape), _resident(ba.shape),
            _resident(wx.shape), _resident(bx.shape), _resident(lam.shape),
        ],
        out_specs=pl.BlockSpec((b, tb, LRU_WIDTH), lambda j: (0, j, 0)),
        out_shape=jax.ShapeDtypeStruct((b, s, LRU_WIDTH), BF16),
        scratch_shapes=[
            pltpu.VMEM((b, SUBLANES + tb, LRU_WIDTH), F32),
            pltpu.VMEM((tb * rows, LANES), F32),
            pltpu.VMEM((tb * rows, LANES), F32),
            pltpu.VMEM((rows, LANES), F32),
        ],
        compiler_params=_cparams(("arbitrary",)),
        name="lru",
    )(xy, conv_w, conv_b, wa, ba, wx, bx, lam)


def _merge_body(alpha, x_ref, odn_ref, os5_ref, olru_ref, gl_ref, wb_ref, bg_ref, wo_ref, g_ref, b_ref, o_ref):
    acc = None
    for n, br_ref in enumerate((odn_ref, os5_ref, olru_ref)):
        branch = _dot(br_ref[...], wb_ref[n])
        gate = _sigmoid(gl_ref[:, n * D_MODEL:(n + 1) * D_MODEL].astype(F32) + bg_ref[n:n + 1, :])
        acc = gate * branch if acc is None else acc + gate * branch
    mixed = _dot(acc.astype(BF16), wo_ref[...])
    y = alpha * x_ref[...] + mixed
    o_ref[...] = _layer_norm(y, g_ref[...], b_ref[...])


def _merge(alpha, x2, odn, os5, olru, gl, wb, bg, wo, g, b):
    t = x2.shape[0]
    tm = min(MERGE_TM, t)
    row = lambda n: pl.BlockSpec((tm, n), lambda i: (i, 0))
    return pl.pallas_call(
        functools.partial(_merge_body, alpha),
        grid=(t // tm,),
        in_specs=[row(D_MODEL), row(DN_WIDTH), row(S5_WIDTH), row(LRU_WIDTH), row(N_BRANCH * D_MODEL),
                  _resident(wb.shape), _resident(bg.shape), _resident(wo.shape), _resident(g.shape), _resident(b.shape)],
        out_specs=row(D_MODEL),
        out_shape=jax.ShapeDtypeStruct((t, D_MODEL), F32),
        compiler_params=_cparams(("arbitrary",)),
        name="merge",
    )(x2, odn, os5, olru, gl, wb, bg, wo, g, b)


def _ffn_body(alpha, tiles_per_seq, x_ref, wup_ref, cw_ref, cb_ref, wdn_ref, g_ref, b_ref, o_ref, hbuf, carry, acc):
    tm = x_ref.shape[0]
    nck = D_FF // FFN_CK

    @pl.when(lax.rem(pl.program_id(0), tiles_per_seq) == 0)
    def _():
        carry[...] = jnp.zeros(carry.shape, F32)

    xb = x_ref[...].astype(BF16)
    for jj in range(nck):
        parts = []
        for part in range(2):
            col = part * D_FF + jj * FFN_CK
            slot = part * nck + jj
            buf = 2 * (jj % 2) + part
            hid = _dot(xb, wup_ref[:, col:col + FFN_CK])
            hbuf[buf, 0:SUBLANES, :] = carry[slot]
            hbuf[buf, SUBLANES:SUBLANES + tm, :] = hid
            carry[slot] = hbuf[buf, tm:tm + SUBLANES, :]
            y = cw_ref[2:3, col:col + FFN_CK] * hid + cb_ref[:, col:col + FFN_CK]
            y = y + cw_ref[1:2, col:col + FFN_CK] * hbuf[buf, SUBLANES - 1:SUBLANES - 1 + tm, :]
            y = y + cw_ref[0:1, col:col + FFN_CK] * hbuf[buf, SUBLANES - 2:SUBLANES - 2 + tm, :]
            parts.append(y)
        act = (_gelu(parts[0]) * parts[1]).astype(BF16)
        contrib = _dot(act, wdn_ref[jj * FFN_CK:(jj + 1) * FFN_CK, :])
        if jj == 0:
            acc[...] = contrib
        else:
            acc[...] += contrib
    y = alpha * x_ref[...] + acc[...]
    o_ref[...] = _layer_norm(y, g_ref[...], b_ref[...])


def _ffn(alpha, seq, x2, wup, cw, cb, wdn, g, b):
    t = x2.shape[0]
    tm = min(FFN_TM, seq)
    row = pl.BlockSpec((tm, D_MODEL), lambda i: (i, 0))
    return pl.pallas_call(
        functools.partial(_ffn_body, alpha, seq // tm),
        grid=(t // tm,),
        in_specs=[row, _resident(wup.shape), _resident(cw.shape), _resident(cb.shape), _resident(wdn.shape),
                  _resident(g.shape), _resident(b.shape)],
        out_specs=row,
        out_shape=jax.ShapeDtypeStruct((t, D_MODEL), F32),
        scratch_shapes=[
            pltpu.VMEM((4, SUBLANES + tm, FFN_CK), F32),
            pltpu.VMEM((2 * (D_FF // FFN_CK), SUBLANES, FFN_CK), F32),
            pltpu.VMEM((tm, D_MODEL), F32),
        ],
        compiler_params=_cparams(("arbitrary",)),
        name="ffn",
    )(x2, wup, cw, cb, wdn, g, b)


def _block_diag(blocks):
    n, r, c = blocks.shape
    eye = jnp.eye(n, dtype=blocks.dtype)
    return (blocks[:, :, None, :] * eye[:, None, :, None]).reshape(n * r, n * c)


def _pack_w_in(w_in):
    offs = [0]
    for n in IN_SIZES:
        offs.append(offs[-1] + n)
    seg = lambda i, j: w_in[..., offs[i]:offs[j]]
    small = seg(4, 6)
    pad = jnp.zeros(w_in.shape[:-1] + (LANES - small.shape[-1],), w_in.dtype)
    return jnp.concatenate([seg(0, 4), seg(6, 7), seg(7, 9), seg(9, 10), small, pad], axis=-1).astype(BF16)


def _lane_row(v, offset):
    nl, n = v.shape
    return jnp.zeros((nl, 1, LANES), F32).at[:, 0, offset:offset + n].set(v)


def _s5_weights(bbr, bbi, c_re, c_im, batch):
    nl = bbr.shape[0]
    hg = S5_GROUPS // 2

    def half_b(bb):
        blocks = jnp.swapaxes(bb, -1, -2).reshape(nl * 2, hg, S5_GROUP, S5_STATE)
        return jax.vmap(_block_diag)(blocks).reshape(nl, 2, hg * S5_GROUP, hg * S5_STATE)

    def half_c(cc):
        blocks = jnp.swapaxes(cc, -1, -2).reshape(nl * 2, hg, S5_STATE, S5_GROUP)
        return jax.vmap(_block_diag)(blocks).reshape(nl, 2, hg * S5_STATE, hg * S5_GROUP)

    wb = jnp.concatenate([half_b(bbr), half_b(bbi)], axis=-1).astype(BF16)
    cw = jnp.concatenate([half_c(c_re), half_c(-c_im)], axis=-2).astype(BF16)
    return wb, cw


def kernel(x, w_in, dn_conv_w, dn_a_log, dn_dt_bias, dn_norm_w, s5_lam_re, s5_lam_im, s5_log_step, s5_b_re, s5_b_im, s5_c_re, s5_c_im, s5_d, s5_w_glu, s5_b_glu, lru_conv_w, lru_conv_b, lru_w_a, lru_b_a, lru_w_x, lru_b_x, lru_lam, w_branch, b_gate, w_out, ln1_g, ln1_b, ffn_w_up, ffn_conv_w, ffn_conv_b, ffn_w_down, ln2_g, ln2_b):
    bsz, seq, _ = x.shape
    depth = w_in.shape[0]
    alpha = (2.0 * depth) ** 0.25
    t = bsz * seq

    w_in_p = _pack_w_in(w_in)
    alog_row = _lane_row(dn_a_log, DN_HEADS)
    dtb_row = _lane_row(dn_dt_bias, DN_HEADS)
    lbr, lbi, bbr, bbi = _s5_prep(s5_lam_re, s5_lam_im, s5_log_step, s5_b_re, s5_b_im)
    s5_wb, s5_cw = _s5_weights(bbr, bbi, s5_c_re, s5_c_im, bsz)
    s5_lr = jnp.tile(lbr.reshape(depth, S5_GROUPS // 2, LANES), (1, bsz, 1))
    s5_li = jnp.tile(lbi.reshape(depth, S5_GROUPS // 2, LANES), (1, bsz, 1))
    s5_wg = s5_w_glu.astype(BF16)
    lru_wa = jax.vmap(_block_diag)(lru_w_a).astype(BF16)
    lru_wx = jax.vmap(_block_diag)(lru_w_x).astype(BF16)
    w_branch_b = w_branch.astype(BF16)
    w_out_b = w_out.astype(BF16)
    w_up_b = ffn_w_up.astype(BF16)
    w_dn_b = ffn_w_down.astype(BF16)
    row = lambda v, l: v[l].reshape(1, -1)

    xc = x.reshape(t, D_MODEL)
    for l in range(depth):
        qkvz, s5u, lru_xy, gate_logits, bd = _proj(xc, w_in_p[l])
        o_dn = _deltanet(qkvz.reshape(bsz, seq, -1), bd.reshape(bsz, seq, LANES), dn_conv_w[l], alog_row[l], dtb_row[l],
                         row(dn_norm_w, l))
        o_s5 = _s5(s5u.reshape(bsz, seq, -1), s5_wb[l], s5_cw[l], s5_lr[l], s5_li[l], row(s5_d, l), s5_wg[l], row(s5_b_glu, l))
        o_lru = _lru(lru_xy.reshape(bsz, seq, -1), lru_conv_w[l], row(lru_conv_b, l), lru_wa[l], row(lru_b_a, l),
                     lru_wx[l], row(lru_b_x, l), row(lru_lam, l))
        xc = _merge(alpha, xc, o_dn.reshape(t, -1), o_s5.reshape(t, -1), o_lru.reshape(t, -1), gate_logits,
                    w_branch_b[l], b_gate[l], w_out_b[l], row(ln1_g, l), row(ln1_b, l))
        xc = _ffn(alpha, seq, xc, w_up_b[l], ffn_conv_w[l], row(ffn_conv_b, l), w_dn_b[l], row(ln2_g, l), row(ln2_b, l))
    return xc.reshape(bsz, seq, D_MODEL)
```

```python
import functools
import math

import jax
import jax.numpy as jnp
from jax import lax
from jax.experimental import pallas as pl
from jax.experimental.pallas import tpu as pltpu

F32 = jnp.float32
BF16 = jnp.bfloat16

D_MODEL = 1024
DN_HEADS = 4
DN_HEAD_DIM = 128
DN_WIDTH = DN_HEADS * DN_HEAD_DIM
DN_CHUNK = 64
S5_WIDTH = 512
S5_GROUP = 16
S5_GROUPS = 32
S5_STATE = 64
LRU_WIDTH = 512
LRU_C = 8.0
N_BRANCH = 3
D_FF = 3 * D_MODEL
IN_SIZES = (DN_WIDTH, DN_WIDTH, DN_WIDTH, DN_WIDTH, DN_HEADS, DN_HEADS, S5_WIDTH, LRU_WIDTH, LRU_WIDTH, N_BRANCH * D_MODEL)
LN_EPS = 1e-5
RMS_EPS = 1e-6
L2_EPS = 1e-6

SUBLANES = 8
LANES = 128
VMEM_LIMIT = 56 * 1024 * 1024

PROJ_TM = 512
PROJ_CK = 512
DN_TB = 256
DN_STREAM_LAG = 1
S5_TB = 256
S5_SLAB = 40
LRU_TB = 512
MERGE_TM = 512
FFN_TM = 512
FFN_CK = 512


def _cparams(sem):
    return pltpu.CompilerParams(dimension_semantics=sem, vmem_limit_bytes=VMEM_LIMIT)


class _Layer:
    def __init__(self, stacked, layer):
        self.stacked = stacked
        self.layer = layer
        self.shape = stacked.shape[1:]


def _operand(w):
    return w.stacked if isinstance(w, _Layer) else w


def _resident(w):
    nd = len(w.shape)
    if isinstance(w, _Layer):
        layer = w.layer
        return pl.BlockSpec((None,) + tuple(w.shape), lambda *_: (layer,) + (0,) * nd, pipeline_mode=pl.Buffered(1))
    return pl.BlockSpec(w.shape, lambda *_: (0,) * nd, pipeline_mode=pl.Buffered(1))


def _sigmoid(x):
    return jax.nn.sigmoid(x)


def _gelu(x):
    return jax.nn.gelu(x)


def _softplus(x):
    return jnp.maximum(x, 0.0) + jnp.log1p(jnp.exp(-jnp.abs(x)))


def _dot(a, b):
    return jnp.dot(a, b, preferred_element_type=F32)


def _dot_nt(a, b):
    return lax.dot_general(a, b, (((1,), (1,)), ((), ())), preferred_element_type=F32)


def _layer_norm(y, g, b):
    mu = jnp.mean(y, axis=-1, keepdims=True)
    d = y - mu
    var = jnp.mean(d * d, axis=-1, keepdims=True)
    return d * lax.rsqrt(var + LN_EPS) * g + b


def _proj_body(tiles_per_seq, x_ref, w_ref, dcw_ref, lcw_ref, lcb_ref, qkvz_ref, s5u_ref, lru_ref, gate_ref, bd_ref, hbuf):
    tm = x_ref.shape[0]
    ck = PROJ_CK
    dh = DN_HEAD_DIM

    @pl.when(lax.rem(pl.program_id(0), tiles_per_seq) == 0)
    def _():
        hbuf[:, 0:SUBLANES, :] = jnp.zeros((hbuf.shape[0], SUBLANES, ck), F32)

    xb = x_ref[...].astype(BF16)
    lru_col = 4 * DN_WIDTH + S5_WIDTH

    def conv4(slot, col, cw_ref, c0):
        hbuf[slot, SUBLANES:SUBLANES + tm, :] = _dot(xb, w_ref[:, col:col + ck])
        y = cw_ref[3:4, c0:c0 + ck] * hbuf[slot, SUBLANES:SUBLANES + tm, :]
        for k in range(3):
            off = SUBLANES - 3 + k
            y = y + cw_ref[k:k + 1, c0:c0 + ck] * hbuf[slot, off:off + tm, :]
        hbuf[slot, 0:SUBLANES, :] = hbuf[slot, tm:tm + SUBLANES, :]
        return y

    for part in range(3):
        y = conv4(part, part * DN_WIDTH, dcw_ref, part * DN_WIDTH)
        y = y * _sigmoid(y)
        for h in range(DN_HEADS):
            yh = y[:, h * dh:(h + 1) * dh]
            if part < 2:
                yh = yh * lax.rsqrt(jnp.sum(yh * yh, axis=-1, keepdims=True) + L2_EPS)
            if part == 0:
                yh = yh * (dh ** -0.5)
            qkvz_ref[:, part * DN_WIDTH + h * dh:part * DN_WIDTH + (h + 1) * dh] = yh.astype(qkvz_ref.dtype)
    lru_ref[:, 0:LRU_WIDTH] = (conv4(3, lru_col, lcw_ref, 0) + lcb_ref[...]).astype(lru_ref.dtype)

    def plain(ref, c, col):
        ref[:, c:c + ck] = _dot(xb, w_ref[:, col:col + ck]).astype(ref.dtype)

    plain(qkvz_ref, 3 * DN_WIDTH, 3 * DN_WIDTH)
    plain(s5u_ref, 0, 4 * DN_WIDTH)
    plain(lru_ref, LRU_WIDTH, lru_col + LRU_WIDTH)
    gate_col = lru_col + 2 * LRU_WIDTH
    for c in range(0, N_BRANCH * D_MODEL, ck):
        plain(gate_ref, c, gate_col + c)
    bd_ref[...] = _dot(xb, w_ref[:, gate_col + N_BRANCH * D_MODEL:gate_col + N_BRANCH * D_MODEL + LANES])


def _proj(seq, x2, w_packed, dn_conv_w, lru_conv_w, lru_conv_b):
    t = x2.shape[0]
    tm = min(PROJ_TM, seq)
    widths = (4 * DN_WIDTH, S5_WIDTH, 2 * LRU_WIDTH, N_BRANCH * D_MODEL)
    out_shape = [jax.ShapeDtypeStruct((t, n), BF16) for n in widths] + [jax.ShapeDtypeStruct((t, LANES), F32)]
    out_specs = [pl.BlockSpec((tm, n), lambda i: (i, 0)) for n in widths] + [pl.BlockSpec((tm, LANES), lambda i: (i, 0))]
    return pl.pallas_call(
        functools.partial(_proj_body, seq // tm),
        grid=(t // tm,),
        in_specs=[pl.BlockSpec((tm, D_MODEL), lambda i: (i, 0)), _resident(w_packed), _resident(dn_conv_w),
                  _resident(lru_conv_w), _resident(lru_conv_b)],
        out_specs=out_specs,
        out_shape=out_shape,
        scratch_shapes=[pltpu.VMEM((4, SUBLANES + tm, PROJ_CK), F32)],
        compiler_params=_cparams(("arbitrary",)),
        name="proj",
    )(x2, *map(_operand, (w_packed, dn_conv_w, lru_conv_w, lru_conv_b)))


def _run_staggered(stage_lists):
    n_stage = len(stage_lists[0])
    lag = DN_STREAM_LAG
    for slot in range(n_stage + lag * (len(stage_lists) - 1)):
        active = [stage_lists[i][slot - lag * i] for i in range(len(stage_lists)) if 0 <= slot - lag * i < n_stage]
        for group in range(max(len(a) for a in active)):
            for a in active:
                if group < len(a):
                    a[group]()


def _dn_body(qkvz_ref, bd_ref, alog_ref, dtb_ref, nw_ref, o_ref, s_ref):
    nb, tb, _ = qkvz_ref.shape
    nchunk = tb // DN_CHUNK
    dh = DN_HEAD_DIM
    heads = range(DN_HEADS)

    @pl.when(pl.program_id(0) == 0)
    def _():
        s_ref[...] = jnp.zeros(s_ref.shape, F32)

    r = lax.broadcasted_iota(jnp.int32, (tb, tb), 0)
    c = lax.broadcasted_iota(jnp.int32, (tb, tb), 1)
    same = (r >> 6) == (c >> 6)
    incl = same & (r >= c)
    strict = same & (r > c)
    eye = (r == c).astype(F32)
    upper = (same & (r <= c)).astype(F32)
    same_f = same.astype(F32)

    def stream(b):
        st = dict(m_pow=[], q_sum=[], rhs=[], attn=[], qd=[], kt_t=[], cds=[])

        def decay_prep():
            bd = bd_ref[b]
            g = -jnp.exp(alog_ref[...]) * _softplus(bd + dtb_ref[...])
            g_t = g.T
            hp = lax.Precision.HIGHEST
            st["gc_row"] = jnp.dot(g_t, upper, preferred_element_type=F32, precision=hp)
            gl_row = jnp.dot(g_t, same_f, preferred_element_type=F32, precision=hp)
            st["gc_col"] = st["gc_row"].T
            st["gl_col"] = gl_row.T
            st["bd"] = bd

        def head_prep(h):
            qb = qkvz_ref[b, :, h * dh:(h + 1) * dh]
            kbf = qkvz_ref[b, :, DN_WIDTH + h * dh:DN_WIDTH + (h + 1) * dh]
            qh, kh = qb.astype(F32), kbf.astype(F32)
            vh = qkvz_ref[b, :, 2 * DN_WIDTH + h * dh:2 * DN_WIDTH + (h + 1) * dh].astype(F32)
            beta = _sigmoid(st["bd"][:, h:h + 1])
            gcol = st["gc_col"][:, DN_HEADS + h:DN_HEADS + h + 1]
            glcol = st["gl_col"][:, DN_HEADS + h:DN_HEADS + h + 1]
            grow = st["gc_row"][DN_HEADS + h:DN_HEADS + h + 1, :]
            decay = jnp.exp(jnp.where(incl, gcol - grow, -1e30))
            egc = jnp.exp(gcol)
            kb = kh * beta
            m1 = jnp.where(strict, _dot_nt(kb.astype(BF16), kbf) * -decay, 0.0)
            st["attn"].append((_dot_nt(qb, kbf) * decay).astype(BF16))
            st["m_pow"].append(m1.astype(BF16))
            st["q_sum"].append(eye + m1)
            st["rhs"].append(jnp.concatenate([vh * beta, kb * egc], axis=1).astype(BF16))
            st["qd"].append((qh * egc).astype(BF16))
            st["kt_t"].append((kh * jnp.exp(glcol - gcol)).T.astype(BF16))
            st["cds"].append(jnp.exp(glcol))

        def inv_square():
            st["m_pow"] = [_dot(m, m).astype(BF16) for m in st["m_pow"]]

        def inv_double():
            m_pow, q_sum = st["m_pow"], st["q_sum"]
            rr = [_dot(m_pow[h], jnp.concatenate([m_pow[h], q_sum[h].astype(BF16)], axis=1)) for h in heads]
            st["m_pow"] = [rr[h][:, :tb].astype(BF16) for h in heads]
            st["q_sum"] = [q_sum[h] + rr[h][:, tb:] for h in heads]

        def inv_finish():
            q_sum = [st["q_sum"][h] + _dot(st["m_pow"][h], st["q_sum"][h].astype(BF16)) for h in heads]
            st["uw"] = [_dot(q_sum[h].astype(BF16), st["rhs"][h]) for h in heads]
            st["state"] = [s_ref[b, h] for h in heads]
            st["outs"] = [[] for _ in heads]

        def chunk_read(ci):
            lo = ci * DN_CHUNK
            hi = lo + DN_CHUNK
            st["ws"] = [_dot(jnp.concatenate([st["uw"][h][lo:hi, dh:].astype(BF16), st["qd"][h][lo:hi]], axis=0),
                             st["state"][h].astype(BF16)) for h in heads]

        def chunk_write(ci):
            lo = ci * DN_CHUNK
            hi = lo + DN_CHUNK
            v_full = []
            for h in heads:
                v_new = st["uw"][h][lo:hi, :dh] - st["ws"][h][:DN_CHUNK]
                pieces = []
                if lo > 0:
                    pieces.append(jnp.zeros((lo, dh), F32))
                pieces.append(v_new)
                if hi < tb:
                    pieces.append(jnp.zeros((tb - hi, dh), F32))
                v_full.append(jnp.concatenate(pieces, axis=0).astype(BF16))
            kv = [_dot(st["kt_t"][h], v_full[h]) for h in heads]
            av = [_dot(st["attn"][h][lo:hi, :], v_full[h]) for h in heads]
            for h in heads:
                st["state"][h] = st["state"][h] * st["cds"][h][lo:lo + 1, :] + kv[h]
                st["outs"][h].append(st["ws"][h][DN_CHUNK:] + av[h])

        def head_out(h):
            s_ref[b, h] = st["state"][h]
            o = jnp.concatenate(st["outs"][h], axis=0)
            o = o * lax.rsqrt(jnp.mean(o * o, axis=-1, keepdims=True) + RMS_EPS) * nw_ref[...]
            z = qkvz_ref[b, :, 3 * DN_WIDTH + h * dh:3 * DN_WIDTH + (h + 1) * dh].astype(F32)
            o_ref[b, :, h * dh:(h + 1) * dh] = (o * (z * _sigmoid(z))).astype(o_ref.dtype)

        part = functools.partial
        return [
            [decay_prep] + [part(head_prep, h) for h in heads],
            [inv_square] + [inv_double] * 4 + [inv_finish],
            [part(step, ci) for ci in range(nchunk) for step in (chunk_read, chunk_write)],
            [part(head_out, h) for h in heads],
        ]

    _run_staggered([stream(b) for b in range(nb)])


def _deltanet(qkvz, bd, alog_row, dtb_row, norm_w):
    b, s, _ = qkvz.shape
    tb = min(DN_TB, s)
    return pl.pallas_call(
        _dn_body,
        grid=(s // tb,),
        in_specs=[
            pl.BlockSpec((b, tb, 4 * DN_WIDTH), lambda j: (0, j, 0)),
            pl.BlockSpec((b, tb, LANES), lambda j: (0, j, 0)),
            _resident(alog_row), _resident(dtb_row), _resident(norm_w),
        ],
        out_specs=pl.BlockSpec((b, tb, DN_WIDTH), lambda j: (0, j, 0)),
        out_shape=jax.ShapeDtypeStruct((b, s, DN_WIDTH), BF16),
        scratch_shapes=[pltpu.VMEM((b, DN_HEADS, DN_HEAD_DIM, DN_HEAD_DIM), F32)],
        compiler_params=_cparams(("arbitrary",)),
        name="dnet",
    )(qkvz, bd, *map(_operand, (alog_row, dtb_row, norm_w)))


def _s5_prep_body(lre_ref, lim_ref, ls_ref, bre_ref, bim_ref, exp_ref, lbr_ref, lbi_ref, bbr_ref, bbi_ref):
    lre = lre_ref[...]
    lim = lim_ref[...]
    delta = jnp.exp(ls_ref[...])
    mag = jnp.exp(lre * delta)
    lbr = mag * jnp.cos(lim * delta)
    lbi = mag * jnp.sin(lim * delta)
    lbr_ref[...] = lbr
    lbi_ref[...] = lbi
    nr = lbr - 1.0
    den = lre * lre + lim * lim
    cr = (nr * lre + lbi * lim) / den
    ci = (lbi * lre - nr * lim) / den
    hp = lax.Precision.HIGHEST
    cr_x = jnp.dot(cr, exp_ref[...], preferred_element_type=F32, precision=hp)
    ci_x = jnp.dot(ci, exp_ref[...], preferred_element_type=F32, precision=hp)
    bre = bre_ref[...]
    bim = bim_ref[...]
    bbr_ref[...] = cr_x * bre - ci_x * bim
    bbi_ref[...] = cr_x * bim + ci_x * bre


def _s5_prep(lam_re, lam_im, log_step, b_re, b_im):
    nl = lam_re.shape[0]
    rows = nl * S5_GROUPS
    lre = lam_re.reshape(rows, S5_STATE)
    lim = lam_im.reshape(rows, S5_STATE)
    ls = log_step.reshape(rows, 1)
    bre = b_re.reshape(rows, S5_STATE * S5_GROUP)
    bim = b_im.reshape(rows, S5_STATE * S5_GROUP)
    expand = jnp.repeat(jnp.eye(S5_STATE, dtype=F32), S5_GROUP, axis=1)
    outs = pl.pallas_call(
        _s5_prep_body,
        out_shape=[jax.ShapeDtypeStruct((rows, S5_STATE), F32)] * 2 + [jax.ShapeDtypeStruct((rows, S5_STATE * S5_GROUP), F32)] * 2,
        name="s5_prep",
    )(lre, lim, ls, bre, bim, expand)
    lbr, lbi, bbr, bbi = outs
    shp = (nl, S5_GROUPS, S5_STATE)
    return lbr.reshape(shp), lbi.reshape(shp), bbr.reshape(shp + (S5_GROUP,)), bbi.reshape(shp + (S5_GROUP,))


def _s5_body(u_ref, wb_ref, cw_ref, lr_ref, li_ref, d_ref, wg_ref, bg_ref, o_ref, sre, sim, hre, him):
    nb, tb, _ = u_ref.shape
    half = S5_WIDTH // 2
    ntile = 8

    @pl.when(pl.program_id(0) == 0)
    def _():
        hre[...] = jnp.zeros(hre.shape, F32)
        him[...] = jnp.zeros(him.shape, F32)

    for b in range(nb):
        for hf in range(2):
            bu = _dot(u_ref[b, :, hf * half:(hf + 1) * half], wb_ref[hf])
            for k in range(ntile):
                row = b * 2 * ntile + hf * ntile + k
                sre[pl.ds(row, tb, stride=S5_SLAB), :] = bu[:, k * LANES:(k + 1) * LANES]
                sim[pl.ds(row, tb, stride=S5_SLAB), :] = bu[:, (ntile + k) * LANES:(ntile + k + 1) * LANES]

    lr = lr_ref[...]
    li = li_ref[...]
    nrow = nb * 2 * ntile

    def step(t, carry):
        hr, hi = carry
        base = pl.multiple_of(t * S5_SLAB, SUBLANES)
        br = sre[pl.ds(base, nrow), :]
        bi = sim[pl.ds(base, nrow), :]
        nr = lr * hr - li * hi + br
        ni = lr * hi + li * hr + bi
        sre[pl.ds(base, nrow), :] = nr
        sim[pl.ds(base, nrow), :] = ni
        return nr, ni

    hr, hi = lax.fori_loop(0, tb, step, (hre[...], him[...]), unroll=4)
    hre[...] = hr
    him[...] = hi

    for b in range(nb):
        ys = []
        for hf in range(2):
            tiles = [sre[pl.ds(b * 2 * ntile + hf * ntile + k, tb, stride=S5_SLAB), :] for k in range(ntile)]
            tiles += [sim[pl.ds(b * 2 * ntile + hf * ntile + k, tb, stride=S5_SLAB), :] for k in range(ntile)]
            hcat = jnp.concatenate(tiles, axis=1).astype(BF16)
            ys.append(_dot(hcat, cw_ref[hf]))
        y = jnp.concatenate(ys, axis=1) + d_ref[...] * u_ref[b].astype(F32)
        y = _gelu(y)
        gate = _sigmoid(_dot(y.astype(BF16), wg_ref[...]) + bg_ref[...])
        o_ref[b] = (y * gate).astype(o_ref.dtype)


def _s5(u, wb, cw, lr, li, d_row, w_glu, b_glu):
    b, s, _ = u.shape
    tb = min(S5_TB, s)
    nrow = b * 16
    return pl.pallas_call(
        _s5_body,
        grid=(s // tb,),
        in_specs=[
            pl.BlockSpec((b, tb, S5_WIDTH), lambda j: (0, j, 0)),
            _resident(wb), _resident(cw), _resident(lr), _resident(li),
            _resident(d_row), _resident(w_glu), _resident(b_glu),
        ],
        out_specs=pl.BlockSpec((b, tb, S5_WIDTH), lambda j: (0, j, 0)),
        out_shape=jax.ShapeDtypeStruct((b, s, S5_WIDTH), BF16),
        scratch_shapes=[
            pltpu.VMEM((tb * S5_SLAB, LANES), F32),
            pltpu.VMEM((tb * S5_SLAB, LANES), F32),
            pltpu.VMEM((nrow, LANES), F32),
            pltpu.VMEM((nrow, LANES), F32),
        ],
        compiler_params=_cparams(("arbitrary",)),
        name="s5",
    )(u, *map(_operand, (wb, cw, lr, li, d_row, w_glu, b_glu)))


def _lru_body(xy_ref, wa_ref, ba_ref, wx_ref, bx_ref, lam_ref, o_ref, a_s, b_s, hcar):
    nb, tb, _ = xy_ref.shape
    w = LRU_WIDTH
    ntile = w // LANES
    rows_per_step = nb * ntile
    first_step = pl.program_id(0) == 0

    @pl.when(first_step)
    def _():
        hcar[...] = jnp.zeros(hcar.shape, F32)

    lam = lam_ref[...]
    log_sig = -_softplus(-lam)
    is_first = first_step & (lax.broadcasted_iota(jnp.int32, (tb, 1), 0) == 0)
    for b in range(nb):
        xcb = xy_ref[b, :, 0:w]
        xc = xcb.astype(F32)
        rg = _sigmoid(_dot(xcb, wa_ref[...]) + ba_ref[...])
        ig = _sigmoid(_dot(xcb, wx_ref[...]) + bx_ref[...])
        log_a = LRU_C * rg * log_sig
        a = jnp.exp(log_a)
        mult = jnp.sqrt(1.0 - a * a)
        mult = jnp.where(is_first, 1.0, mult)
        bb = mult * ig * xc
        for ct in range(ntile):
            a_s[pl.ds(b * ntile + ct, tb, stride=rows_per_step), :] = a[:, ct * LANES:(ct + 1) * LANES]
            b_s[pl.ds(b * ntile + ct, tb, stride=rows_per_step), :] = bb[:, ct * LANES:(ct + 1) * LANES]

    def step(t, h):
        base = pl.multiple_of(t * rows_per_step, rows_per_step)
        h = a_s[pl.ds(base, rows_per_step), :] * h + b_s[pl.ds(base, rows_per_step), :]
        b_s[pl.ds(base, rows_per_step), :] = h
        return h

    hcar[...] = lax.fori_loop(0, tb, step, hcar[...], unroll=8)

    for b in range(nb):
        for ct in range(ntile):
            hseq = b_s[pl.ds(b * ntile + ct, tb, stride=rows_per_step), :]
            y = xy_ref[b, :, w + ct * LANES:w + (ct + 1) * LANES].astype(F32)
            o_ref[b, :, ct * LANES:(ct + 1) * LANES] = (hseq * _gelu(y)).astype(o_ref.dtype)


def _lru(xy, wa, ba, wx, bx, lam):
    b, s, _ = xy.shape
    tb = min(LRU_TB, s)
    rows = b * (LRU_WIDTH // LANES)
    return pl.pallas_call(
        _lru_body,
        grid=(s // tb,),
        in_specs=[
            pl.BlockSpec((b, tb, 2 * LRU_WIDTH), lambda j: (0, j, 0)),
            _resident(wa), _resident(ba), _resident(wx), _resident(bx), _resident(lam),
        ],
        out_specs=pl.BlockSpec((b, tb, LRU_WIDTH), lambda j: (0, j, 0)),
        out_shape=jax.ShapeDtypeStruct((b, s, LRU_WIDTH), BF16),
        scratch_shapes=[
            pltpu.VMEM((tb * rows, LANES), F32),
            pltpu.VMEM((tb * rows, LANES), F32),
            pltpu.VMEM((rows, LANES), F32),
        ],
        compiler_params=_cparams(("arbitrary",)),
        name="lru",
    )(xy, *map(_operand, (wa, ba, wx, bx, lam)))


def _merge_body(alpha, x_ref, odn_ref, os5_ref, olru_ref, gl_ref, wb_ref, bg_ref, wo_ref, g_ref, b_ref, o_ref):
    acc = None
    for n, br_ref in enumerate((odn_ref, os5_ref, olru_ref)):
        branch = _dot(br_ref[...], wb_ref[n])
        gate = _sigmoid(gl_ref[:, n * D_MODEL:(n + 1) * D_MODEL].astype(F32) + bg_ref[n:n + 1, :])
        acc = gate * branch if acc is None else acc + gate * branch
    mixed = _dot(acc.astype(BF16), wo_ref[...])
    y = alpha * x_ref[...] + mixed
    o_ref[...] = _layer_norm(y, g_ref[...], b_ref[...])


def _merge(alpha, x2, odn, os5, olru, gl, wb, bg, wo, g, b):
    t = x2.shape[0]
    tm = min(MERGE_TM, t)
    row = lambda n: pl.BlockSpec((tm, n), lambda i: (i, 0))
    return pl.pallas_call(
        functools.partial(_merge_body, alpha),
        grid=(t // tm,),
        in_specs=[row(D_MODEL), row(DN_WIDTH), row(S5_WIDTH), row(LRU_WIDTH), row(N_BRANCH * D_MODEL),
                  _resident(wb), _resident(bg), _resident(wo), _resident(g), _resident(b)],
        out_specs=row(D_MODEL),
        out_shape=jax.ShapeDtypeStruct((t, D_MODEL), F32),
        compiler_params=_cparams(("arbitrary",)),
        name="merge",
    )(x2, odn, os5, olru, gl, *map(_operand, (wb, bg, wo, g, b)))


def _interleave_perm(tm):
    r = jnp.arange(tm)
    src = (r % SUBLANES) * (tm // SUBLANES) + r // SUBLANES
    return (src[:, None] == jnp.arange(tm)[None, :]).astype(BF16)


def _ffn_body(alpha, tiles_per_seq, x_ref, perm_ref, wup_ref, cw_ref, cb_ref, wdn_ref, g_ref, b_ref, o_ref, hbuf, carry, acc):
    tm = x_ref.shape[0]
    seg = tm // SUBLANES
    nck = D_FF // FFN_CK
    nlt = D_MODEL // LANES
    hist = 2 * SUBLANES

    @pl.when(lax.rem(pl.program_id(0), tiles_per_seq) == 0)
    def _():
        carry[...] = jnp.zeros(carry.shape, F32)

    xp = _dot(perm_ref[...], x_ref[...].astype(BF16)).astype(BF16)
    row0 = lax.broadcasted_iota(jnp.int32, (SUBLANES, FFN_CK), 0) == 0

    def wrap(cur, prev):
        return jnp.where(row0, pltpu.roll(prev, 1, 0), pltpu.roll(cur, 1, 0))

    def up(jj):
        for part in range(2):
            col = part * D_FF + jj * FFN_CK
            slot = part * nck + jj
            buf = 2 * (jj % 2) + part
            hbuf[buf, hist:hist + tm, :] = _dot(xp, wup_ref[:, col:col + FFN_CK])
            tail = hbuf[buf, tm:tm + hist, :]
            prev = carry[slot]
            hbuf[buf, 0:SUBLANES, :] = wrap(tail[:SUBLANES], prev[:SUBLANES])
            hbuf[buf, SUBLANES:hist, :] = wrap(tail[SUBLANES:], prev[SUBLANES:])
            carry[slot] = tail

    up(0)
    for jj in range(nck):
        if jj + 1 < nck:
            up(jj + 1)
        parts = []
        for part in range(2):
            col = part * D_FF + jj * FFN_CK
            buf = 2 * (jj % 2) + part
            y = cw_ref[2:3, col:col + FFN_CK] * hbuf[buf, hist:hist + tm, :] + cb_ref[:, col:col + FFN_CK]
            y = y + cw_ref[1:2, col:col + FFN_CK] * hbuf[buf, SUBLANES:SUBLANES + tm, :]
            y = y + cw_ref[0:1, col:col + FFN_CK] * hbuf[buf, 0:tm, :]
            parts.append(y)
        act = (_gelu(parts[0]) * parts[1]).astype(BF16)
        contrib = _dot(act, wdn_ref[jj * FFN_CK:(jj + 1) * FFN_CK, :])
        for c in range(nlt):
            piece = contrib[:, c * LANES:(c + 1) * LANES]
            if jj == 0:
                acc[c] = piece
            else:
                acc[c] += piece
    for j in range(SUBLANES):
        rows = slice(j * seg, (j + 1) * seg)
        f = jnp.concatenate([acc[c, pl.ds(j, seg, stride=SUBLANES), :] for c in range(nlt)], axis=1)
        y = alpha * x_ref[rows, :] + f
        o_ref[rows, :] = _layer_norm(y, g_ref[...], b_ref[...])


def _ffn(alpha, seq, x2, wup, cw, cb, wdn, g, b):
    t = x2.shape[0]
    tm = min(FFN_TM, seq)
    perm = _interleave_perm(tm)
    row = pl.BlockSpec((tm, D_MODEL), lambda i: (i, 0))
    return pl.pallas_call(
        functools.partial(_ffn_body, alpha, seq // tm),
        grid=(t // tm,),
        in_specs=[row, _resident(perm), _resident(wup), _resident(cw), _resident(cb),
                  _resident(wdn), _resident(g), _resident(b)],
        out_specs=row,
        out_shape=jax.ShapeDtypeStruct((t, D_MODEL), F32),
        scratch_shapes=[
            pltpu.VMEM((4, 2 * SUBLANES + tm, FFN_CK), F32),
            pltpu.VMEM((2 * (D_FF // FFN_CK), 2 * SUBLANES, FFN_CK), F32),
            pltpu.VMEM((D_MODEL // LANES, tm, LANES), F32),
        ],
        compiler_params=_cparams(("arbitrary",)),
        name="ffn",
    )(x2, perm, *map(_operand, (wup, cw, cb, wdn, g, b)))


def _block_diag(blocks):
    n, r, c = blocks.shape
    eye = jnp.eye(n, dtype=blocks.dtype)
    return (blocks[:, :, None, :] * eye[:, None, :, None]).reshape(n * r, n * c)


def _pack_w_in(w_in):
    offs = [0]
    for n in IN_SIZES:
        offs.append(offs[-1] + n)
    seg = lambda i, j: w_in[..., offs[i]:offs[j]].astype(BF16)
    small = seg(4, 6)
    pad = jnp.zeros(w_in.shape[:-1] + (LANES - small.shape[-1],), BF16)
    return jnp.concatenate([seg(0, 4), seg(6, 7), seg(7, 9), seg(9, 10), small, pad], axis=-1)


def _lane_row(v, offset):
    nl, n = v.shape
    return jnp.zeros((nl, 1, LANES), F32).at[:, 0, offset:offset + n].set(v)


def _s5_weights(bbr, bbi, c_re, c_im, batch):
    nl = bbr.shape[0]
    hg = S5_GROUPS // 2

    def half_b(bb):
        blocks = jnp.swapaxes(bb, -1, -2).reshape(nl * 2, hg, S5_GROUP, S5_STATE)
        return jax.vmap(_block_diag)(blocks).reshape(nl, 2, hg * S5_GROUP, hg * S5_STATE)

    def half_c(cc):
        blocks = jnp.swapaxes(cc, -1, -2).reshape(nl * 2, hg, S5_STATE, S5_GROUP)
        return jax.vmap(_block_diag)(blocks).reshape(nl, 2, hg * S5_STATE, hg * S5_GROUP)

    wb = jnp.concatenate([half_b(bbr), half_b(bbi)], axis=-1).astype(BF16)
    cw = jnp.concatenate([half_c(c_re), half_c(-c_im)], axis=-2).astype(BF16)
    return wb, cw


def kernel(x, w_in, dn_conv_w, dn_a_log, dn_dt_bias, dn_norm_w, s5_lam_re, s5_lam_im, s5_log_step, s5_b_re, s5_b_im, s5_c_re, s5_c_im, s5_d, s5_w_glu, s5_b_glu, lru_conv_w, lru_conv_b, lru_w_a, lru_b_a, lru_w_x, lru_b_x, lru_lam, w_branch, b_gate, w_out, ln1_g, ln1_b, ffn_w_up, ffn_conv_w, ffn_conv_b, ffn_w_down, ln2_g, ln2_b):
    bsz, seq, _ = x.shape
    depth = w_in.shape[0]
    alpha = (2.0 * depth) ** 0.25
    t = bsz * seq

    w_in_p = _pack_w_in(w_in)
    alog_row = _lane_row(dn_a_log, DN_HEADS)
    dtb_row = _lane_row(dn_dt_bias, DN_HEADS)
    lbr, lbi, bbr, bbi = _s5_prep(s5_lam_re, s5_lam_im, s5_log_step, s5_b_re, s5_b_im)
    s5_wb, s5_cw = _s5_weights(bbr, bbi, s5_c_re, s5_c_im, bsz)
    s5_lr = jnp.tile(lbr.reshape(depth, S5_GROUPS // 2, LANES), (1, bsz, 1))
    s5_li = jnp.tile(lbi.reshape(depth, S5_GROUPS // 2, LANES), (1, bsz, 1))
    s5_wg = s5_w_glu.astype(BF16)
    lru_wa = jax.vmap(_block_diag)(lru_w_a).astype(BF16)
    lru_wx = jax.vmap(_block_diag)(lru_w_x).astype(BF16)
    w_branch_b = w_branch.astype(BF16)
    w_out_b = w_out.astype(BF16)
    w_up_b = ffn_w_up.astype(BF16)
    w_dn_b = ffn_w_down.astype(BF16)
    vec = lambda v: v.reshape(depth, 1, -1)
    dn_norm_r, s5_d_r, s5_bg_r = vec(dn_norm_w), vec(s5_d), vec(s5_b_glu)
    lru_cb_r, lru_ba_r, lru_bx_r, lru_lam_r = vec(lru_conv_b), vec(lru_b_a), vec(lru_b_x), vec(lru_lam)
    ln1_g_r, ln1_b_r, ln2_g_r, ln2_b_r, ffn_cb_r = vec(ln1_g), vec(ln1_b), vec(ln2_g), vec(ln2_b), vec(ffn_conv_b)

    xc = x.reshape(t, D_MODEL)
    for l in range(depth):
        lay = functools.partial(_Layer, layer=l)
        qkvz, s5u, lru_xy, gate_logits, bd = _proj(seq, xc, lay(w_in_p), lay(dn_conv_w), lay(lru_conv_w), lay(lru_cb_r))
        o_dn = _deltanet(qkvz.reshape(bsz, seq, -1), bd.reshape(bsz, seq, LANES), lay(alog_row), lay(dtb_row), lay(dn_norm_r))
        o_s5 = _s5(s5u.reshape(bsz, seq, -1), lay(s5_wb), lay(s5_cw), lay(s5_lr), lay(s5_li), lay(s5_d_r), lay(s5_wg),
                   lay(s5_bg_r))
        o_lru = _lru(lru_xy.reshape(bsz, seq, -1), lay(lru_wa), lay(lru_ba_r), lay(lru_wx), lay(lru_bx_r), lay(lru_lam_r))
        xc = _merge(alpha, xc, o_dn.reshape(t, -1), o_s5.reshape(t, -1), o_lru.reshape(t, -1), gate_logits,
                    lay(w_branch_b), lay(b_gate), lay(w_out_b), lay(ln1_g_r), lay(ln1_b_r))
        xc = _ffn(alpha, seq, xc, lay(w_up_b), lay(ffn_conv_w), lay(ffn_cb_r), lay(w_dn_b), lay(ln2_g_r), lay(ln2_b_r))
    return xc.reshape(bsz, seq, D_MODEL)
```
